```python
import math
import jax, jax.numpy as jnp
from jax import lax
import numpy as np

D_MODEL = 2048
BATCH = 1
SEQ = 16384
DEPTH = 4
DEC_BATCH = 4
DEC_SEQ = 4096
PAST_LEN = 128

HEAD_DIM = 128
N_HEADS = D_MODEL // HEAD_DIM
N_KV_HEADS = N_HEADS // 4
KV_GROUP = N_HEADS // N_KV_HEADS
WINDOW = 128
BLOCK = 128
ROPE_THETA = 10000.0
HY_WIDTH = D_MODEL // 2
HY_ORDER = 2
HY_BANDS = 16
HY_POS_DIM = 1 + 2 * HY_BANDS
HY_FFN = 64
HY_MIN_DECAY = math.log(1e-2) / 1.5
HY_MAX_DECAY = math.log(1e-2) / 0.3
S5_WIDTH = D_MODEL // 2
S5_GROUP_CH = 16
S5_GROUPS = S5_WIDTH // S5_GROUP_CH
S5_STATE = 64
N_EXPERTS = 16
EC_FACTOR = 2
EXPERT_FF = ((8 * D_MODEL // 3 + 127) // 128) * 128
Q_DIM = N_HEADS * HEAD_DIM
KV_DIM = N_KV_HEADS * HEAD_DIM
HY_IN_DIM = (HY_ORDER + 1) * HY_WIDTH
GATE_DIM = 3 * D_MODEL
IN_DIM = Q_DIM + 2 * KV_DIM + HY_IN_DIM + S5_WIDTH + GATE_DIM
MIX_DIM = Q_DIM + HY_WIDTH + S5_WIDTH
SPLIT_POINTS = (Q_DIM, Q_DIM + KV_DIM, Q_DIM + 2 * KV_DIM,
                Q_DIM + 2 * KV_DIM + HY_IN_DIM, Q_DIM + 2 * KV_DIM + HY_IN_DIM + S5_WIDTH)
RMS_EPS = 1e-6
NEG_INF = -1e30

kernel_name = "hybrid_bidir_attn_hyena_s5_ec_moe"


def rms_norm(x, g):
    xf = x.astype(jnp.float32)
    y = xf * lax.rsqrt(jnp.mean(xf * xf, axis=-1, keepdims=True) + RMS_EPS)
    return (y * g.astype(jnp.float32)).astype(x.dtype)


def rope_tables(L):
    inv = 1.0 / (ROPE_THETA ** (jnp.arange(0, HEAD_DIM, 2, dtype=jnp.float32) / HEAD_DIM))
    ang = jnp.arange(L, dtype=jnp.float32)[:, None] * inv[None, :]
    ang = jnp.concatenate([ang, ang], axis=-1)
    return jnp.cos(ang)[None, :, None, :], jnp.sin(ang)[None, :, None, :]


def apply_rope(x, cos, sin):
    xf = x.astype(jnp.float32)
    x1, x2 = jnp.split(xf, 2, axis=-1)
    rot = jnp.concatenate([-x2, x1], axis=-1)
    return (xf * cos + rot * sin).astype(x.dtype)


def _band(t, nb):
    B = t.shape[0]
    tp = jnp.pad(t, ((0, 0), (BLOCK, BLOCK), (0, 0), (0, 0))).reshape(B, nb + 2, BLOCK, N_KV_HEADS, HEAD_DIM)
    return jnp.concatenate([tp[:, :-2], tp[:, 1:-1], tp[:, 2:]], axis=2)


def band_mask(L):
    nb = L // BLOCK
    qpos = jnp.arange(L).reshape(nb, BLOCK)[:, :, None]
    kpos = (jnp.arange(nb)[:, None] * BLOCK + jnp.arange(3 * BLOCK)[None, :] - BLOCK)[:, None, :]
    return (jnp.abs(kpos - qpos) <= WINDOW) & (kpos >= 0) & (kpos < L)


def windowed_gqa(q, k, v, sink):
    B, L = q.shape[0], q.shape[1]
    nb = L // BLOCK
    qb = q.reshape(B, nb, BLOCK, N_KV_HEADS, KV_GROUP, HEAD_DIM)
    kb, vb = _band(k, nb), _band(v, nb)
    s = jnp.einsum('bnqhgd,bnkhd->bnhgqk', qb, kb, preferred_element_type=jnp.float32) * (HEAD_DIM ** -0.5)
    mask = band_mask(L)[None, :, None, None]
    s = jnp.where(mask, s, NEG_INF)
    sk = sink.astype(jnp.float32).reshape(1, 1, N_KV_HEADS, KV_GROUP, 1, 1)
    m = jnp.maximum(jnp.max(s, axis=-1, keepdims=True), sk)
    p = jnp.exp(s - m)
    probs = p / (jnp.sum(p, axis=-1, keepdims=True) + jnp.exp(sk - m))
    o = jnp.einsum('bnhgqk,bnkhd->bnqhgd', probs.astype(v.dtype), vb)
    return o.reshape(B, L, N_HEADS * HEAD_DIM)


def short_conv3(x, w, b):
    xp = jnp.pad(x, ((0, 0), (1, 1), (0, 0)))
    return xp[:, :-2] * w[0] + xp[:, 1:-1] * w[1] + xp[:, 2:] * w[2] + b


def hyena_filter_spectrum(L, w1, b1, w2, b2, w3, freq):
    f32 = jnp.float32
    t = jnp.linspace(0.0, 1.0, L, dtype=f32)[:, None]
    bands = jnp.linspace(1e-4, HY_BANDS - 1, HY_BANDS, dtype=f32)[None, :]
    fw = bands * (2.0 * jnp.pi * jnp.arange(L, dtype=f32) / L)[:, None]
    feats = jnp.concatenate([t, jnp.cos(fw), -jnp.sin(fw)], axis=-1)
    hdn = jnp.sin(freq[0].astype(f32) * (feats @ w1.astype(f32) + b1.astype(f32)))
    hdn = jnp.sin(freq[1].astype(f32) * (hdn @ w2.astype(f32) + b2.astype(f32)))
    filt = (hdn @ w3.astype(f32)).reshape(L, HY_ORDER, 2, HY_WIDTH)
    deltas = jnp.linspace(HY_MIN_DECAY, HY_MAX_DECAY, HY_WIDTH, dtype=f32)
    filt = filt * jnp.exp(-t * jnp.abs(deltas))[:, None, None, :]
    fwd, bwd = filt[:, :, 0], filt[:, :, 1]
    k2 = jnp.concatenate([fwd, jnp.zeros((1, HY_ORDER, HY_WIDTH), f32), bwd[:0:-1]], axis=0)
    k2 = k2 / jnp.sum(jnp.abs(k2), axis=0, keepdims=True)
    return jnp.fft.rfft(k2, axis=0)


def fft_long_conv(z, spec, bias):
    L = z.shape[1]
    Z = jnp.fft.rfft(z, n=2 * L, axis=1)
    y = jnp.fft.irfft(Z * spec[None], n=2 * L, axis=1)[:, :L]
    return y + z * bias.astype(jnp.float32)


def hyena_branch(u, conv_w, conv_b, w1, b1, w2, b2, w3, freq, bias):
    L = u.shape[1]
    uc = short_conv3(u, conv_w, conv_b).astype(jnp.float32)
    x1, x2, z = jnp.split(uc, 3, axis=-1)
    spec = hyena_filter_spectrum(L, w1, b1, w2, b2, w3, freq)
    for n, gate in enumerate((x1, x2)):
        z = gate * fft_long_conv(z, spec[:, n], bias[n])
    return z.astype(u.dtype)


def _ssm_combine(e1, e2):
    a1, b1 = e1
    a2, b2 = e2
    return a2 * a1, a2 * b1 + b2


def s5_branch(u, a_re, a_im, log_dt, b_re, b_im, c_re, c_im, d_skip, glu_w, glu_b):
    f32 = jnp.float32
    B, L, W = u.shape
    uf = u.astype(f32)
    ucx = uf.reshape(B, L, S5_GROUPS, S5_GROUP_CH).astype(jnp.complex64)
    y = jnp.zeros((B, L, S5_GROUPS, S5_GROUP_CH), f32)
    for d in range(2):
        lam = lax.complex(a_re[d].astype(f32), a_im[d].astype(f32))
        dt = jnp.exp(log_dt[d].astype(f32))[:, None]
        lam_bar = jnp.exp(lam * dt)
        b_bar = ((lam_bar - 1.0) / lam)[..., None] * lax.complex(b_re[d].astype(f32), b_im[d].astype(f32))
        bu = jnp.einsum('blgc,gpc->blgp', ucx, b_bar)
        if d == 1:
            bu = bu[:, ::-1]
        _, xs = lax.associative_scan(_ssm_combine, (jnp.broadcast_to(lam_bar, bu.shape), bu), axis=1)
        if d == 1:
            xs = xs[:, ::-1]
        c = lax.complex(c_re[d].astype(f32), c_im[d].astype(f32))
        y = y + jnp.einsum('blgp,gcp->blgc', xs, c).real
    y = y.reshape(B, L, W) + d_skip.astype(f32) * uf
    y = jax.nn.gelu(y)
    y = y * jax.nn.sigmoid(y @ glu_w.astype(f32) + glu_b.astype(f32))
    return y.astype(u.dtype)


def expert_choice_ffn(h, w_router, w_gate_e, w_up_e, w_down_e):
    B, L, D = h.shape
    T = B * L
    cap = EC_FACTOR * T // N_EXPERTS
    hf = h.reshape(T, D)
    aff = jax.nn.softmax(jnp.einsum('td,de->te', hf, w_router, preferred_element_type=jnp.float32), axis=-1)
    gate, idx = lax.top_k(aff.T, cap)
    xe = hf[idx]
    a = jnp.einsum('ecd,edf->ecf', xe, w_gate_e)
    b = jnp.einsum('ecd,edf->ecf', xe, w_up_e)
    ye = jnp.einsum('ecf,efd->ecd', jax.nn.silu(a) * b, w_down_e)
    ye = ye * gate[..., None].astype(ye.dtype)
    out = jnp.zeros_like(hf).at[idx.reshape(-1)].add(ye.reshape(-1, D))
    return out.reshape(B, L, D)


def encoder_layer(x, norm1_g, w_in, q_norm_g, k_norm_g, attn_sink, hy_conv_w, hy_conv_b, hy_ffn_w1, hy_ffn_b1,
                  hy_ffn_w2, hy_ffn_b2, hy_ffn_w3, hy_sin_freq, hy_bias, s5_a_re, s5_a_im, s5_log_dt, s5_b_re,
                  s5_b_im, s5_c_re, s5_c_im, s5_d, s5_glu_w, s5_glu_b, w_branch, w_out, norm2_g, w_router,
                  w_gate_e, w_up_e, w_down_e):
    B, L, _ = x.shape
    h = rms_norm(x, norm1_g)
    proj = jnp.einsum('bld,de->ble', h, w_in)
    q, k, v, hy_in, s5_in, gate_in = jnp.split(proj, SPLIT_POINTS, axis=-1)
    cos, sin = rope_tables(L)
    q = apply_rope(rms_norm(q.reshape(B, L, N_HEADS, HEAD_DIM), q_norm_g), cos, sin)
    k = apply_rope(rms_norm(k.reshape(B, L, N_KV_HEADS, HEAD_DIM), k_norm_g), cos, sin)
    v = v.reshape(B, L, N_KV_HEADS, HEAD_DIM)
    o_a = windowed_gqa(q, k, v, attn_sink)
    o_b = hyena_branch(hy_in, hy_conv_w, hy_conv_b, hy_ffn_w1, hy_ffn_b1, hy_ffn_w2, hy_ffn_b2,
                       hy_ffn_w3, hy_sin_freq, hy_bias)
    o_c = s5_branch(s5_in, s5_a_re, s5_a_im, s5_log_dt, s5_b_re, s5_b_im, s5_c_re, s5_c_im,
                    s5_d, s5_glu_w, s5_glu_b)
    u_a = o_a @ w_branch[:Q_DIM]
    u_b = o_b @ w_branch[Q_DIM:Q_DIM + HY_WIDTH]
    u_c = o_c @ w_branch[Q_DIM + HY_WIDTH:]
    g = jax.nn.sigmoid(gate_in).reshape(B, L, 3, D_MODEL)
    merged = g[:, :, 0] * u_a + g[:, :, 1] * u_b + g[:, :, 2] * u_c
    x = x + merged @ w_out
    h = rms_norm(x, norm2_g)
    return x + expert_choice_ffn(h, w_router, w_gate_e, w_up_e, w_down_e)


def setup_inputs(seed: int = 0) -> dict:
    key = jax.random.key(seed)
    ks = iter(jax.random.split(key, 40))
    nrm = lambda shape, scale: jax.random.normal(next(ks), shape, jnp.float32) * scale
    gain = lambda shape: 1.0 + nrm(shape, 0.02)
    w_branch = jnp.concatenate([nrm((DEPTH, Q_DIM, D_MODEL), Q_DIM ** -0.5),
                                nrm((DEPTH, HY_WIDTH, D_MODEL), HY_WIDTH ** -0.5),
                                nrm((DEPTH, S5_WIDTH, D_MODEL), S5_WIDTH ** -0.5)], axis=1)
    s5_a_re = -0.5 + nrm((DEPTH, 2, S5_GROUPS, S5_STATE), 0.01)
    s5_a_im = jnp.pi * jnp.arange(S5_STATE, dtype=jnp.float32) + nrm((DEPTH, 2, S5_GROUPS, S5_STATE), 0.01)
    s5_log_dt = jax.random.uniform(next(ks), (DEPTH, 2, S5_GROUPS), jnp.float32,
                                   minval=math.log(1e-3), maxval=math.log(1e-1))
    return {
        "x_prompt": nrm((BATCH, SEQ, D_MODEL), 1.0),
        "x_sample": nrm((DEC_BATCH, DEC_SEQ, D_MODEL), 1.0),
        "norm1_g": gain((DEPTH, D_MODEL)),
        "w_in": nrm((DEPTH, D_MODEL, IN_DIM), D_MODEL ** -0.5),
        "q_norm_g": gain((DEPTH, HEAD_DIM)),
        "k_norm_g": gain((DEPTH, HEAD_DIM)),
        "attn_sink": nrm((DEPTH, N_HEADS), 0.5),
        "hy_conv_w": nrm((DEPTH, 3, HY_IN_DIM), 3 ** -0.5),
        "hy_conv_b": nrm((DEPTH, HY_IN_DIM), 0.02),
        "hy_ffn_w1": nrm((DEPTH, HY_POS_DIM, HY_FFN), HY_POS_DIM ** -0.5),
        "hy_ffn_b1": nrm((DEPTH, HY_FFN), 0.02),
        "hy_ffn_w2": nrm((DEPTH, HY_FFN, HY_FFN), HY_FFN ** -0.5),
        "hy_ffn_b2": nrm((DEPTH, HY_FFN), 0.02),
        "hy_ffn_w3": nrm((DEPTH, HY_FFN, HY_ORDER * 2 * HY_WIDTH), HY_FFN ** -0.5),
        "hy_sin_freq": gain((DEPTH, 2, HY_FFN)),
        "hy_bias": nrm((DEPTH, HY_ORDER, HY_WIDTH), 0.02),
        "s5_a_re": s5_a_re,
        "s5_a_im": s5_a_im,
        "s5_log_dt": s5_log_dt,
        "s5_b_re": nrm((DEPTH, 2, S5_GROUPS, S5_STATE, S5_GROUP_CH), (2 * S5_GROUP_CH) ** -0.5),
        "s5_b_im": nrm((DEPTH, 2, S5_GROUPS, S5_STATE, S5_GROUP_CH), (2 * S5_GROUP_CH) ** -0.5),
        "s5_c_re": nrm((DEPTH, 2, S5_GROUPS, S5_GROUP_CH, S5_STATE), (2 * S5_STATE) ** -0.5),
        "s5_c_im": nrm((DEPTH, 2, S5_GROUPS, S5_GROUP_CH, S5_STATE), (2 * S5_STATE) ** -0.5),
        "s5_d": nrm((DEPTH, S5_WIDTH), 1.0),
        "s5_glu_w": nrm((DEPTH, S5_WIDTH, S5_WIDTH), S5_WIDTH ** -0.5),
        "s5_glu_b": nrm((DEPTH, S5_WIDTH), 0.02),
        "w_branch": w_branch,
        "w_out": nrm((DEPTH, D_MODEL, D_MODEL), D_MODEL ** -0.5),
        "norm2_g": gain((DEPTH, D_MODEL)),
        "w_router": nrm((DEPTH, D_MODEL, N_EXPERTS), D_MODEL ** -0.5),
        "w_gate_e": nrm((DEPTH, N_EXPERTS, D_MODEL, EXPERT_FF), D_MODEL ** -0.5),
        "w_up_e": nrm((DEPTH, N_EXPERTS, D_MODEL, EXPERT_FF), D_MODEL ** -0.5),
        "w_down_e": nrm((DEPTH, N_EXPERTS, EXPERT_FF, D_MODEL), EXPERT_FF ** -0.5),
    }


def reference(x_prompt, x_sample, norm1_g, w_in, q_norm_g, k_norm_g, attn_sink, hy_conv_w, hy_conv_b,
              hy_ffn_w1, hy_ffn_b1, hy_ffn_w2, hy_ffn_b2, hy_ffn_w3, hy_sin_freq, hy_bias, s5_a_re, s5_a_im,
              s5_log_dt, s5_b_re, s5_b_im, s5_c_re, s5_c_im, s5_d, s5_glu_w, s5_glu_b, w_branch, w_out,
              norm2_g, w_router, w_gate_e, w_up_e, w_down_e):
    layer_weights = (norm1_g, w_in, q_norm_g, k_norm_g, attn_sink, hy_conv_w, hy_conv_b, hy_ffn_w1, hy_ffn_b1,
                     hy_ffn_w2, hy_ffn_b2, hy_ffn_w3, hy_sin_freq, hy_bias, s5_a_re, s5_a_im, s5_log_dt,
                     s5_b_re, s5_b_im, s5_c_re, s5_c_im, s5_d, s5_glu_w, s5_glu_b, w_branch, w_out,
                     norm2_g, w_router, w_gate_e, w_up_e, w_down_e)
    y_prompt = x_prompt
    y_sample = x_sample
    for l in range(DEPTH):
        lw = [w[l] for w in layer_weights]
        y_prompt = encoder_layer(y_prompt, *lw)
        y_sample = encoder_layer(y_sample, *lw)
    return (y_prompt, y_sample)
```

```python
import math
from functools import partial

import jax
import jax.numpy as jnp
import numpy as np
from jax import lax
from jax.experimental import pallas as pl
from jax.experimental.pallas import tpu as pltpu

D_MODEL = 2048
SEQ = 16384
DEPTH = 4
DEC_BATCH = 4
DEC_SEQ = 4096
T_GROUP = 16384
T_ALL = 2 * T_GROUP
SEQ_LENS = (SEQ,) + (DEC_SEQ,) * DEC_BATCH

HEAD_DIM = 128
N_HEADS = 16
N_KV_HEADS = 4
KV_GROUP = 4
WINDOW = 128
BLOCK = 128
ROPE_THETA = 10000.0
HY_WIDTH = 1024
HY_ORDER = 2
HY_BANDS = 16
HY_FFN = 64
HY_MIN_DECAY = math.log(1e-2) / 1.5
HY_MAX_DECAY = math.log(1e-2) / 0.3
HY_DFT_PROMPT = (256, 128)
HY_DFT_SAMPLE = (64, 128)
S5_WIDTH = 1024
S5_GROUP_CH = 16
S5_GROUPS = 64
S5_STATE = 64
N_EXPERTS = 16
EC_FACTOR = 2
EXPERT_FF = 5504
EXPERT_FF_PAD = 5632
Q_DIM = 2048
KV_DIM = 512
HY_IN_DIM = 3072
GATE_DIM = 3 * D_MODEL
IN_DIM = Q_DIM + 2 * KV_DIM + HY_IN_DIM + S5_WIDTH + GATE_DIM
K_OFF = Q_DIM
V_OFF = Q_DIM + KV_DIM
HY_OFF = Q_DIM + 2 * KV_DIM
S5_OFF = HY_OFF + HY_IN_DIM
GATE_OFF = S5_OFF + S5_WIDTH
RMS_EPS = 1e-6
NEG_INF = -1e30

VMEM_LIMIT = 56 * 1024 * 1024
F32 = jnp.float32
BF16 = jnp.bfloat16


def _cparams(sem):
    return pltpu.CompilerParams(dimension_semantics=sem, vmem_limit_bytes=VMEM_LIMIT)


def _rmsnorm_kernel(x_ref, g_ref, o_ref):
    x = x_ref[...]
    y = x * lax.rsqrt(jnp.mean(x * x, axis=-1, keepdims=True) + RMS_EPS)
    o_ref[...] = (y * g_ref[...]).astype(o_ref.dtype)


def rmsnorm_bf16(x, g, tm=512):
    T, D = x.shape
    return pl.pallas_call(
        _rmsnorm_kernel,
        grid=(T // tm,),
        in_specs=[pl.BlockSpec((tm, D), lambda i: (i, 0)), pl.BlockSpec((1, D), lambda i: (0, 0))],
        out_specs=pl.BlockSpec((tm, D), lambda i: (i, 0)),
        out_shape=jax.ShapeDtypeStruct((T, D), BF16),
        compiler_params=_cparams(("parallel",)),
        name="rmsnorm",
    )(x, g.reshape(1, D))


def _mm_kernel(a_ref, b_ref, o_ref):
    o_ref[...] = jnp.dot(a_ref[...], b_ref[...], preferred_element_type=F32).astype(o_ref.dtype)


def _mm_res_kernel(a_ref, b_ref, r_ref, o_ref):
    o_ref[...] = r_ref[...] + jnp.dot(a_ref[...], b_ref[...], preferred_element_type=F32)


def matmul(a, b, out_dtype, tm, tn, residual=None, name="matmul"):
    M, K = a.shape
    N = b.shape[1]
    in_specs = [pl.BlockSpec((tm, K), lambda i, j: (i, 0)), pl.BlockSpec((K, tn), lambda i, j: (0, j))]
    args = [a, b]
    body = _mm_kernel
    if residual is not None:
        in_specs.append(pl.BlockSpec((tm, tn), lambda i, j: (i, j)))
        args.append(residual)
        body = _mm_res_kernel
    return pl.pallas_call(
        body,
        grid=(M // tm, N // tn),
        in_specs=in_specs,
        out_specs=pl.BlockSpec((tm, tn), lambda i, j: (i, j)),
        out_shape=jax.ShapeDtypeStruct((M, N), out_dtype),
        compiler_params=_cparams(("parallel", "parallel")),
        name=name,
    )(*args)


def _pos_block(i, tm):
    pb = T_GROUP // tm
    sb = DEC_SEQ // tm
    return jnp.where(i < pb, i, (i - pb) % sb)


def _headnorm_rope_kernel(x_ref, g_ref, cos_ref, sin_ref, o_ref, *, n_heads):
    cos = cos_ref[...]
    sin = sin_ref[...]
    g = g_ref[...]
    for h in range(n_heads):
        x = x_ref[:, h * HEAD_DIM:(h + 1) * HEAD_DIM]
        y = x * lax.rsqrt(jnp.mean(x * x, axis=-1, keepdims=True) + RMS_EPS) * g
        o = y * cos + pltpu.roll(y, HEAD_DIM // 2, axis=1) * sin
        o_ref[:, h * HEAD_DIM:(h + 1) * HEAD_DIM] = o.astype(o_ref.dtype)


def headnorm_rope(proj, col_block, n_heads, g, cos, sin_signed, tm=512):
    T = proj.shape[0]
    W = n_heads * HEAD_DIM
    return pl.pallas_call(
        partial(_headnorm_rope_kernel, n_heads=n_heads),
        grid=(T // tm,),
        in_specs=[pl.BlockSpec((tm, W), lambda i: (i, col_block)),
                  pl.BlockSpec((1, HEAD_DIM), lambda i: (0, 0)),
                  pl.BlockSpec((tm, HEAD_DIM), lambda i: (_pos_block(i, tm), 0)),
                  pl.BlockSpec((tm, HEAD_DIM), lambda i: (_pos_block(i, tm), 0))],
        out_specs=pl.BlockSpec((tm, W), lambda i: (i, 0)),
        out_shape=jax.ShapeDtypeStruct((T, W), BF16),
        compiler_params=_cparams(("parallel",)),
        name="headnorm_rope",
    )(proj, g.reshape(1, HEAD_DIM), cos, sin_signed)


def _attn_kernel(sink_ref, q_ref, kp_ref, kc_ref, kn_ref, vp_ref, vc_ref, vn_ref, o_ref):
    i = pl.program_id(0)
    pb = T_GROUP // BLOCK
    sb = DEC_SEQ // BLOCK
    seq_first = jnp.where(i < pb, 0, pb + ((i - pb) // sb) * sb)
    seq_last = jnp.where(i < pb, pb - 1, seq_first + sb - 1)
    has_prev = i > seq_first
    has_next = i < seq_last

    rows = lax.broadcasted_iota(jnp.int32, (KV_GROUP * BLOCK, 3 * BLOCK), 0) % BLOCK
    cols = lax.broadcasted_iota(jnp.int32, (KV_GROUP * BLOCK, 3 * BLOCK), 1)
    rel = cols - BLOCK - rows
    mask = (jnp.abs(rel) <= WINDOW)
    mask = mask & ((cols >= BLOCK) | has_prev) & ((cols < 2 * BLOCK) | has_next)
    scale = HEAD_DIM ** -0.5

    for h in range(N_KV_HEADS):
        sl = slice(h * HEAD_DIM, (h + 1) * HEAD_DIM)
        kk = jnp.concatenate([kp_ref[:, sl], kc_ref[:, sl], kn_ref[:, sl]], axis=0)
        vv = jnp.concatenate([vp_ref[:, sl], vc_ref[:, sl], vn_ref[:, sl]], axis=0).astype(BF16)
        qg = jnp.concatenate(
            [q_ref[:, (h * KV_GROUP + g) * HEAD_DIM:(h * KV_GROUP + g + 1) * HEAD_DIM] for g in range(KV_GROUP)],
            axis=0)
        s = lax.dot_general(qg, kk, (((1,), (1,)), ((), ())), preferred_element_type=F32) * scale
        s = jnp.where(mask, s, NEG_INF)
        sk = jnp.concatenate(
            [jnp.full((BLOCK, 1), sink_ref[h * KV_GROUP + g], F32) for g in range(KV_GROUP)], axis=0)
        m = jnp.maximum(jnp.max(s, axis=-1, keepdims=True), sk)
        p = jnp.exp(s - m)
        probs = p / (jnp.sum(p, axis=-1, keepdims=True) + jnp.exp(sk - m))
        o = jnp.dot(probs.astype(BF16), vv, preferred_element_type=F32)
        for g in range(KV_GROUP):
            hd = h * KV_GROUP + g
            o_ref[:, hd * HEAD_DIM:(hd + 1) * HEAD_DIM] = o[g * BLOCK:(g + 1) * BLOCK].astype(o_ref.dtype)


def windowed_attention(qh, kh, proj, sink):
    T = qh.shape[0]
    nb = T // BLOCK
    v_blk = V_OFF // KV_DIM
    prev = lambda i, s: (jnp.maximum(i - 1, 0), 0)
    cur = lambda i, s: (i, 0)
    nxt = lambda i, s: (jnp.minimum(i + 1, nb - 1), 0)
    vprev = lambda i, s: (jnp.maximum(i - 1, 0), v_blk)
    vcur = lambda i, s: (i, v_blk)
    vnxt = lambda i, s: (jnp.minimum(i + 1, nb - 1), v_blk)
    grid_spec = pltpu.PrefetchScalarGridSpec(
        num_scalar_prefetch=1,
        grid=(nb,),
        in_specs=[pl.BlockSpec((BLOCK, Q_DIM), cur),
                  pl.BlockSpec((BLOCK, KV_DIM), prev),
                  pl.BlockSpec((BLOCK, KV_DIM), cur),
                  pl.BlockSpec((BLOCK, KV_DIM), nxt),
                  pl.BlockSpec((BLOCK, KV_DIM), vprev),
                  pl.BlockSpec((BLOCK, KV_DIM), vcur),
                  pl.BlockSpec((BLOCK, KV_DIM), vnxt)],
        out_specs=pl.BlockSpec((BLOCK, Q_DIM), cur),
    )
    return pl.pallas_call(
        _attn_kernel,
        grid_spec=grid_spec,
        out_shape=jax.ShapeDtypeStruct((T, Q_DIM), BF16),
        compiler_params=_cparams(("parallel",)),
        name="windowed_attention",
    )(sink, qh, kh, kh, kh, proj, proj, proj)


def _merge_kernel(oa_ref, ob_ref, oc_ref, wa_ref, wb_ref, wc_ref, ga_ref, gb_ref, gc_ref, o_ref):
    ua = jnp.dot(oa_ref[...], wa_ref[...], preferred_element_type=F32)
    ub = jnp.dot(ob_ref[...], wb_ref[...], preferred_element_type=F32)
    uc = jnp.dot(oc_ref[...], wc_ref[...], preferred_element_type=F32)
    m = jax.nn.sigmoid(ga_ref[...]) * ua + jax.nn.sigmoid(gb_ref[...]) * ub + jax.nn.sigmoid(gc_ref[...]) * uc
    o_ref[...] = m.astype(o_ref.dtype)


def gated_merge(o_a, o_b, o_c, w_branch, proj, tm=1024, tn=512):
    T = o_a.shape[0]
    gb0 = GATE_OFF // tn
    nd = D_MODEL // tn
    wb_blk = Q_DIM // HY_WIDTH
    return pl.pallas_call(
        _merge_kernel,
        grid=(T // tm, nd),
        in_specs=[pl.BlockSpec((tm, Q_DIM), lambda i, j: (i, 0)),
                  pl.BlockSpec((tm, HY_WIDTH), lambda i, j: (i, 0)),
                  pl.BlockSpec((tm, S5_WIDTH), lambda i, j: (i, 0)),
                  pl.BlockSpec((Q_DIM, tn), lambda i, j: (0, j)),
                  pl.BlockSpec((HY_WIDTH, tn), lambda i, j: (wb_blk, j)),
                  pl.BlockSpec((S5_WIDTH, tn), lambda i, j: (wb_blk + 1, j)),
                  pl.BlockSpec((tm, tn), lambda i, j: (i, gb0 + j)),
                  pl.BlockSpec((tm, tn), lambda i, j: (i, gb0 + nd + j)),
                  pl.BlockSpec((tm, tn), lambda i, j: (i, gb0 + 2 * nd + j))],
        out_specs=pl.BlockSpec((tm, tn), lambda i, j: (i, j)),
        out_shape=jax.ShapeDtypeStruct((T, D_MODEL), BF16),
        compiler_params=_cparams(("parallel", "parallel")),
        name="gated_merge",
    )(o_a, o_b, o_c, w_branch, w_branch, w_branch, proj, proj, proj)


def _expert_kernel(x_ref, wg_ref, wu_ref, wd_ref, o_ref):
    f = pl.program_id(2)
    x = x_ref[0]
    a = jnp.dot(x, wg_ref[0], preferred_element_type=F32)
    b = jnp.dot(x, wu_ref[0], preferred_element_type=F32)
    act = (a * jax.nn.sigmoid(a) * b).astype(BF16)
    y = jnp.dot(act, wd_ref[0], preferred_element_type=F32)

    @pl.when(f == 0)
    def _():
        o_ref[0] = y

    @pl.when(f != 0)
    def _():
        o_ref[0] += y


def expert_ffn(xe, wg, wu, wd, tm=1024, tf=512):
    E, C, D = xe.shape
    F = wg.shape[2]
    return pl.pallas_call(
        _expert_kernel,
        grid=(E, C // tm, F // tf),
        in_specs=[pl.BlockSpec((1, tm, D), lambda e, i, f: (e, i, 0)),
                  pl.BlockSpec((1, D, tf), lambda e, i, f: (e, 0, f)),
                  pl.BlockSpec((1, D, tf), lambda e, i, f: (e, 0, f)),
                  pl.BlockSpec((1, tf, D), lambda e, i, f: (e, f, 0))],
        out_specs=pl.BlockSpec((1, tm, D), lambda e, i, f: (e, i, 0)),
        out_shape=jax.ShapeDtypeStruct((E, C, D), F32),
        compiler_params=_cparams(("parallel", "parallel", "arbitrary")),
        name="expert_ffn",
    )(xe, wg, wu, wd)


def _short_conv3(x, w, b):
    xp = jnp.pad(x, ((0, 0), (1, 1), (0, 0)))
    return xp[:, :-2] * w[0] + xp[:, 1:-1] * w[1] + xp[:, 2:] * w[2] + b


def _hyena_filter_spectrum(L, w1, b1, w2, b2, w3, freq):
    t = jnp.linspace(0.0, 1.0, L, dtype=F32)[:, None]
    bands = jnp.linspace(1e-4, HY_BANDS - 1, HY_BANDS, dtype=F32)[None, :]
    fw = bands * (2.0 * jnp.pi * jnp.arange(L, dtype=F32) / L)[:, None]
    feats = jnp.concatenate([t, jnp.cos(fw), -jnp.sin(fw)], axis=-1)
    hdn = jnp.sin(freq[0] * (feats @ w1 + b1))
    hdn = jnp.sin(freq[1] * (hdn @ w2 + b2))
    filt = (hdn @ w3).reshape(L, HY_ORDER, 2, HY_WIDTH)
    deltas = jnp.linspace(HY_MIN_DECAY, HY_MAX_DECAY, HY_WIDTH, dtype=F32)
    filt = filt * jnp.exp(-t * jnp.abs(deltas))[:, None, None, :]
    fwd, bwd = filt[:, :, 0], filt[:, :, 1]
    k2 = jnp.concatenate([fwd, jnp.zeros((1, HY_ORDER, HY_WIDTH), F32), bwd[:0:-1]], axis=0)
    k2 = k2 / jnp.sum(jnp.abs(k2), axis=0, keepdims=True)
    return jnp.fft.rfft(k2, axis=0)


def _dft_tables(n1_pts, n2_pts):
    n = n1_pts * n2_pts
    k1 = jnp.arange(n1_pts, dtype=jnp.int32)
    m1 = jnp.arange(n1_pts // 2, dtype=jnp.int32)
    ang1 = (2.0 * math.pi / n1_pts) * ((k1[:, None] * m1[None, :]) % n1_pts).astype(F32)
    c1, s1 = jnp.cos(ang1), jnp.sin(ang1)
    f_fwd = jnp.concatenate([c1, -s1], axis=0)
    f_inv = jnp.concatenate([c1.T, -s1.T], axis=1)
    k2 = jnp.arange(n2_pts, dtype=jnp.int32)
    m2 = jnp.arange(n2_pts, dtype=jnp.int32)
    idx = (m2[None, None, :] * (n1_pts * k2[None, :, None] + k1[:, None, None])) % n
    ang = (2.0 * math.pi / n) * idx.astype(F32)
    cg, sg = jnp.cos(ang), jnp.sin(ang)
    g_fwd = jnp.concatenate([jnp.concatenate([cg, sg], axis=2),
                             jnp.concatenate([-sg, cg], axis=2)], axis=1)
    cgt, sgt = cg.transpose(0, 2, 1), sg.transpose(0, 2, 1)
    g_inv = jnp.concatenate([jnp.concatenate([cgt, -sgt], axis=2),
                             jnp.concatenate([sgt, cgt], axis=2)], axis=1)
    return f_fwd.astype(BF16), f_inv.astype(BF16), g_fwd.astype(BF16), g_inv.astype(BF16)


def _spectrum_layout(spec, n1_pts, n2_pts):
    L = spec.shape[0] - 1
    n = 2 * L
    full = jnp.concatenate([spec, jnp.conj(spec[L - 1:0:-1])], axis=0) / n
    full = full.reshape(n2_pts, n1_pts, -1).transpose(1, 0, 2)
    return jnp.stack([full.real, full.imag], axis=0)


def _hy_stage1_kernel(z_ref, f_ref, a_ref):
    a_ref[0] = jnp.dot(f_ref[...], z_ref[0].astype(BF16), preferred_element_type=F32).astype(a_ref.dtype)


def _hy_mid_kernel(a_ref, h_ref, g_ref, gi_ref, b_ref, *, kb, n2_pts):
    for j in range(kb):
        a = jnp.concatenate([a_ref[0, 0, j], a_ref[0, 1, j]], axis=0)
        z = jnp.dot(g_ref[j], a, preferred_element_type=F32)
        zre, zim = z[:n2_pts], z[n2_pts:]
        hre, him = h_ref[0, j], h_ref[1, j]
        y = jnp.concatenate([zre * hre - zim * him, zre * him + zim * hre], axis=0).astype(BF16)
        b = jnp.dot(gi_ref[j], y, preferred_element_type=F32)
        b_ref[0, 0, j] = b[:n2_pts].astype(b_ref.dtype)
        b_ref[0, 1, j] = b[n2_pts:].astype(b_ref.dtype)


def _hy_stage3_kernel(b_ref, fi_ref, z_ref, gate_ref, bias_ref, o_ref):
    y = jnp.dot(fi_ref[...], b_ref[0], preferred_element_type=F32)
    o_ref[0] = (gate_ref[0] * (y + z_ref[0] * bias_ref[...])).astype(o_ref.dtype)


def hyena_long_conv(z, gate, spec_l, bias, tables, n1_pts, n2_pts, out_dtype, tc=4096, kb=4):
    B, L, C = z.shape
    f_fwd, f_inv, g_fwd, g_inv = tables
    h1 = n1_pts // 2
    cols = n2_pts * C
    tc = min(tc, cols)
    z2 = z.reshape(B, h1, cols)
    a = pl.pallas_call(
        _hy_stage1_kernel,
        grid=(B, cols // tc),
        in_specs=[pl.BlockSpec((1, h1, tc), lambda b, j: (b, 0, j)),
                  pl.BlockSpec((2 * n1_pts, h1), lambda b, j: (0, 0))],
        out_specs=pl.BlockSpec((1, 2 * n1_pts, tc), lambda b, j: (b, 0, j)),
        out_shape=jax.ShapeDtypeStruct((B, 2 * n1_pts, cols), BF16),
        compiler_params=_cparams(("parallel", "parallel")),
        name="hyena_dft_stage1",
    )(z2, f_fwd)
    a = a.reshape(B, 2, n1_pts, n2_pts, C)
    bmid = pl.pallas_call(
        partial(_hy_mid_kernel, kb=kb, n2_pts=n2_pts),
        grid=(n1_pts // kb, B),
        in_specs=[pl.BlockSpec((1, 2, kb, n2_pts, C), lambda k, b: (b, 0, k, 0, 0)),
                  pl.BlockSpec((2, kb, n2_pts, C), lambda k, b: (0, k, 0, 0)),
                  pl.BlockSpec((kb, 2 * n2_pts, 2 * n2_pts), lambda k, b: (k, 0, 0)),
                  pl.BlockSpec((kb, 2 * n2_pts, 2 * n2_pts), lambda k, b: (k, 0, 0))],
        out_specs=pl.BlockSpec((1, 2, kb, n2_pts, C), lambda k, b: (b, 0, k, 0, 0)),
        out_shape=jax.ShapeDtypeStruct((B, 2, n1_pts, n2_pts, C), BF16),
        compiler_params=_cparams(("parallel", "parallel")),
        name="hyena_dft_mid",
    )(a, spec_l, g_fwd, g_inv)
    bmid = bmid.reshape(B, 2 * n1_pts, cols)
    out = pl.pallas_call(
        _hy_stage3_kernel,
        grid=(B, cols // tc),
        in_specs=[pl.BlockSpec((1, 2 * n1_pts, tc), lambda b, j: (b, 0, j)),
                  pl.BlockSpec((h1, 2 * n1_pts), lambda b, j: (0, 0)),
                  pl.BlockSpec((1, h1, tc), lambda b, j: (b, 0, j)),
                  pl.BlockSpec((1, h1, tc), lambda b, j: (b, 0, j)),
                  pl.BlockSpec((1, tc), lambda b, j: (0, 0))],
        out_specs=pl.BlockSpec((1, h1, tc), lambda b, j: (b, 0, j)),
        out_shape=jax.ShapeDtypeStruct((B, h1, cols), out_dtype),
        compiler_params=_cparams(("parallel", "parallel")),
        name="hyena_dft_stage3",
    )(bmid, f_inv, z2, gate.reshape(B, h1, cols), jnp.tile(bias.reshape(1, C), (1, tc // C)))
    return out.reshape(B, L, C)


def _hyena_branch(u, conv_w, conv_b, w1, b1, w2, b2, w3, freq, bias, n1_pts, n2_pts):
    L = u.shape[1]
    uc = _short_conv3(u, conv_w, conv_b)
    x1, x2, z = jnp.split(uc, 3, axis=-1)
    spec = _hyena_filter_spectrum(L, w1, b1, w2, b2, w3, freq)
    tables = _dft_tables(n1_pts, n2_pts)
    z = hyena_long_conv(z, x1, _spectrum_layout(spec[:, 0], n1_pts, n2_pts), bias[0], tables, n1_pts, n2_pts, F32)
    return hyena_long_conv(z, x2, _spectrum_layout(spec[:, 1], n1_pts, n2_pts), bias[1], tables, n1_pts, n2_pts, BF16)


S5_CHUNK = 64
S5_CW = S5_CHUNK * S5_GROUP_CH
S5_SW = 4 * S5_STATE
HI = lax.Precision.HIGHEST


def _s5_tables(a_re, a_im, log_dt, b_re, b_im, c_re, c_im, n_levels):
    Tc = S5_CHUNK
    lam = lax.complex(a_re, a_im)
    dt = jnp.exp(log_dt)[..., None]
    lam_dt = lam * dt
    lam_bar = jnp.exp(lam_dt)
    b_bar = ((lam_bar - 1.0) / lam)[..., None] * lax.complex(b_re, b_im)
    c = lax.complex(c_re, c_im)
    j = jnp.arange(Tc + 1, dtype=F32)
    pw = jnp.exp(lam_dt[:, None] * j[None, :, None, None])
    kk = jnp.einsum('dgop,djgp,dgpi->djgoi', c, pw[:, :Tc], b_bar, precision=HI).real
    kfull = jnp.concatenate([kk[1, :0:-1], (kk[0, :1] + kk[1, :1]), kk[0, 1:]], axis=0)
    s_idx = jnp.arange(Tc)[:, None]
    t_idx = jnp.arange(Tc)[None, :]
    m = kfull[t_idx - s_idx + Tc - 1]
    m_toep = m.transpose(2, 0, 4, 1, 3).reshape(S5_GROUPS, S5_CW, S5_CW)
    bf = pw[0, Tc - 1 - jnp.arange(Tc)][..., None] * b_bar[0][None]
    bb = pw[1, jnp.arange(Tc)][..., None] * b_bar[1][None]
    to_cols = lambda x: x.transpose(1, 0, 3, 2).reshape(S5_GROUPS, S5_CW, S5_STATE)
    b_cat = jnp.concatenate([to_cols(bf.real), to_cols(bf.imag), to_cols(bb.real), to_cols(bb.imag)], axis=-1)
    wf = c[0][None] * pw[0, 1 + jnp.arange(Tc)][:, :, None, :]
    wb = c[1][None] * pw[1, Tc - jnp.arange(Tc)][:, :, None, :]
    to_rows = lambda x: x.transpose(1, 3, 0, 2).reshape(S5_GROUPS, S5_STATE, S5_CW)
    c_cat = jnp.concatenate([to_rows(wf.real), -to_rows(wf.imag), to_rows(wb.real), -to_rows(wb.imag)], axis=1)
    lv = (2.0 ** jnp.arange(n_levels, dtype=F32)) * Tc
    al = jnp.exp(lam_dt[:, None] * lv[None, :, None, None])
    a1 = jnp.concatenate([al[0].real, al[0].real, al[1].real, al[1].real], axis=-1).transpose(1, 0, 2)
    a2 = jnp.concatenate([-al[0].imag, al[0].imag, -al[1].imag, al[1].imag], axis=-1).transpose(1, 0, 2)
    return b_cat.astype(BF16), m_toep.astype(BF16), c_cat.astype(BF16), a1, a2


def _s5_core_kernel(u_ref, bcat_ref, m_ref, ccat_ref, a1_ref, a2_ref, st_ref, en_ref, y_ref, *, n_levels):
    u = u_ref[0]
    nc = u.shape[0]
    z = jnp.dot(u, bcat_ref[0], preferred_element_type=F32)
    half = 2 * S5_STATE
    c = lax.broadcasted_iota(jnp.int32, (nc, half), 0)
    st = st_ref[...]
    en = en_ref[...]
    sf = z[:, :half]
    sb = z[:, half:]
    for k in range(n_levels):
        s = 1 << k
        a1 = a1_ref[0, k:k + 1, :]
        a2 = a2_ref[0, k:k + 1, :]
        shf = pltpu.roll(sf, s, axis=0)
        tf = shf * a1[:, :half] + pltpu.roll(shf, S5_STATE, axis=1) * a2[:, :half]
        sf = sf + jnp.where(c - s >= st, tf, 0.0)
        shb = pltpu.roll(sb, nc - s, axis=0)
        tb = shb * a1[:, half:] + pltpu.roll(shb, S5_STATE, axis=1) * a2[:, half:]
        sb = sb + jnp.where(c + s <= en, tb, 0.0)
    sprev = jnp.where(c - 1 >= st, pltpu.roll(sf, 1, axis=0), 0.0)
    snext = jnp.where(c + 1 <= en, pltpu.roll(sb, nc - 1, axis=0), 0.0)
    carried = jnp.concatenate([sprev, snext], axis=1).astype(BF16)
    y = jnp.dot(u, m_ref[0], preferred_element_type=F32)
    y_ref[0] = y + jnp.dot(carried, ccat_ref[0], preferred_element_type=F32)


def _s5_glu_kernel(y_ref, u_ref, d_ref, w_ref, b_ref, o_ref):
    v = jax.nn.gelu(y_ref[...] + d_ref[...] * u_ref[...])
    t = jnp.dot(v.astype(BF16), w_ref[...], preferred_element_type=F32) + b_ref[...]
    o_ref[...] = (v * jax.nn.sigmoid(t)).astype(o_ref.dtype)


def s5_mixer(src, u_col_block, seq_lens, a_re, a_im, log_dt, b_re, b_im, c_re, c_im, d_skip, glu_w, glu_b, tm=512):
    T = src.shape[0]
    Tc = S5_CHUNK
    nc = T // Tc
    starts, ends = [], []
    off = 0
    for L in seq_lens:
        n = L // Tc
        starts += [off] * n
        ends += [off + n - 1] * n
        off += n
    n_levels = max(1, int(math.ceil(math.log2(max(seq_lens) // Tc))))
    st = jnp.asarray(np.array(starts, np.int32).reshape(nc, 1))
    en = jnp.asarray(np.array(ends, np.int32).reshape(nc, 1))
    b_cat, m_toep, c_cat, a1, a2 = _s5_tables(a_re, a_im, log_dt, b_re, b_im, c_re, c_im, n_levels)

    u = src[:, u_col_block * S5_WIDTH:(u_col_block + 1) * S5_WIDTH]
    ug = u.astype(BF16).reshape(nc, Tc, S5_GROUPS, S5_GROUP_CH).transpose(2, 0, 1, 3).reshape(S5_GROUPS, nc, S5_CW)
    per_g = lambda r, c_: pl.BlockSpec((1, r, c_), lambda g: (g, 0, 0))
    yg = pl.pallas_call(
        partial(_s5_core_kernel, n_levels=n_levels),
        grid=(S5_GROUPS,),
        in_specs=[per_g(nc, S5_CW), per_g(S5_CW, S5_SW), per_g(S5_CW, S5_CW), per_g(S5_SW, S5_CW),
                  per_g(n_levels, S5_SW), per_g(n_levels, S5_SW),
                  pl.BlockSpec((nc, 1), lambda g: (0, 0)), pl.BlockSpec((nc, 1), lambda g: (0, 0))],
        out_specs=per_g(nc, S5_CW),
        out_shape=jax.ShapeDtypeStruct((S5_GROUPS, nc, S5_CW), F32),
        compiler_params=_cparams(("parallel",)),
        name="s5_core",
    )(ug, b_cat, m_toep, c_cat, a1, a2, st, en)
    y = yg.reshape(S5_GROUPS, nc, Tc, S5_GROUP_CH).transpose(1, 2, 0, 3).reshape(T, S5_WIDTH)
    return pl.pallas_call(
        _s5_glu_kernel,
        grid=(T // tm,),
        in_specs=[pl.BlockSpec((tm, S5_WIDTH), lambda i: (i, 0)),
                  pl.BlockSpec((tm, S5_WIDTH), lambda i: (i, u_col_block)),
                  pl.BlockSpec((1, S5_WIDTH), lambda i: (0, 0)),
                  pl.BlockSpec((S5_WIDTH, S5_WIDTH), lambda i: (0, 0)),
                  pl.BlockSpec((1, S5_WIDTH), lambda i: (0, 0))],
        out_specs=pl.BlockSpec((tm, S5_WIDTH), lambda i: (i, 0)),
        out_shape=jax.ShapeDtypeStruct((T, S5_WIDTH), BF16),
        compiler_params=_cparams(("parallel",)),
        name="s5_glu",
    )(y, src, d_skip.reshape(1, S5_WIDTH), glu_w.astype(BF16), glu_b.reshape(1, S5_WIDTH))


def _per_sequence(fn, slab):
    C = slab.shape[1]
    yp = fn(slab[:T_GROUP].reshape(1, SEQ, C))
    ys = fn(slab[T_GROUP:].reshape(DEC_BATCH, DEC_SEQ, C))
    return jnp.concatenate([yp.reshape(T_GROUP, -1), ys.reshape(T_GROUP, -1)], axis=0)


def _route(h2, w_router):
    cap = EC_FACTOR * T_GROUP // N_EXPERTS
    logits = jnp.einsum('td,de->te', h2.astype(F32), w_router, precision=lax.Precision.HIGHEST)
    aff = jax.nn.softmax(logits, axis=-1)
    gate, idx = lax.top_k(aff.T, cap)
    return gate, idx


def _rope_tables():
    inv = 1.0 / (ROPE_THETA ** (jnp.arange(0, HEAD_DIM, 2, dtype=F32) / HEAD_DIM))
    ang = jnp.arange(SEQ, dtype=F32)[:, None] * inv[None, :]
    ang = jnp.concatenate([ang, ang], axis=-1)
    sign = jnp.concatenate([-jnp.ones((HEAD_DIM // 2,), F32), jnp.ones((HEAD_DIM // 2,), F32)])
    return jnp.cos(ang), jnp.sin(ang) * sign[None, :]


def _layer(x, cos, sin_signed, norm1_g, w_in, q_norm_g, k_norm_g, attn_sink, hy_conv_w, hy_conv_b, hy_ffn_w1,
           hy_ffn_b1, hy_ffn_w2, hy_ffn_b2, hy_ffn_w3, hy_sin_freq, hy_bias, s5_a_re, s5_a_im, s5_log_dt, s5_b_re,
           s5_b_im, s5_c_re, s5_c_im, s5_d, s5_glu_w, s5_glu_b, w_branch, w_out, norm2_g, w_router,
           w_gate_e, w_up_e, w_down_e):
    h = rmsnorm_bf16(x, norm1_g)
    proj = matmul(h, w_in.astype(BF16), F32, tm=1024, tn=1024, name="in_proj")

    qh = headnorm_rope(proj, 0, N_HEADS, q_norm_g, cos, sin_signed)
    kh = headnorm_rope(proj, K_OFF // KV_DIM, N_KV_HEADS, k_norm_g, cos, sin_signed)
    o_a = windowed_attention(qh, kh, proj, attn_sink)

    hy_in = proj[:, HY_OFF:HY_OFF + HY_IN_DIM]
    hy_w = (hy_conv_w, hy_conv_b, hy_ffn_w1, hy_ffn_b1, hy_ffn_w2, hy_ffn_b2, hy_ffn_w3, hy_sin_freq, hy_bias)
    o_b = jnp.concatenate(
        [_hyena_branch(hy_in[:T_GROUP].reshape(1, SEQ, HY_IN_DIM), *hy_w, *HY_DFT_PROMPT).reshape(T_GROUP, HY_WIDTH),
         _hyena_branch(hy_in[T_GROUP:].reshape(DEC_BATCH, DEC_SEQ, HY_IN_DIM), *hy_w, *HY_DFT_SAMPLE)
         .reshape(T_GROUP, HY_WIDTH)], axis=0)
    o_c = s5_mixer(proj, S5_OFF // S5_WIDTH, SEQ_LENS, s5_a_re, s5_a_im, s5_log_dt, s5_b_re, s5_b_im, s5_c_re,
                   s5_c_im, s5_d, s5_glu_w, s5_glu_b)

    merged = gated_merge(o_a, o_b, o_c, w_branch.astype(BF16), proj)
    x = matmul(merged, w_out.astype(BF16), F32, tm=1024, tn=1024, residual=x, name="out_proj")

    h2 = rmsnorm_bf16(x, norm2_g)
    pad = ((0, 0), (0, 0), (0, EXPERT_FF_PAD - EXPERT_FF))
    wg = jnp.pad(w_gate_e.astype(BF16), pad)
    wu = jnp.pad(w_up_e.astype(BF16), pad)
    wd = jnp.pad(w_down_e.astype(BF16), ((0, 0), (0, EXPERT_FF_PAD - EXPERT_FF), (0, 0)))

    gates, idxs = [], []
    for grp in range(2):
        gate, idx = _route(h2[grp * T_GROUP:(grp + 1) * T_GROUP], w_router)
        gates.append(gate)
        idxs.append(idx + grp * T_GROUP)
    gate = jnp.concatenate(gates, axis=1)
    idx = jnp.concatenate(idxs, axis=1)
    xe = h2[idx]
    ye = expert_ffn(xe, wg, wu, wd)
    ye = ye * gate[..., None]
    return x.at[idx.reshape(-1)].add(ye.reshape(-1, D_MODEL))


def kernel(x_prompt, x_sample, norm1_g, w_in, q_norm_g, k_norm_g, attn_sink, hy_conv_w, hy_conv_b, hy_ffn_w1, hy_ffn_b1, hy_ffn_w2, hy_ffn_b2, hy_ffn_w3, hy_sin_freq, hy_bias, s5_a_re, s5_a_im, s5_log_dt, s5_b_re, s5_b_im, s5_c_re, s5_c_im, s5_d, s5_glu_w, s5_glu_b, w_branch, w_out, norm2_g, w_router, w_gate_e, w_up_e, w_down_e):
    layer_weights = (norm1_g, w_in, q_norm_g, k_norm_g, attn_sink, hy_conv_w, hy_conv_b, hy_ffn_w1, hy_ffn_b1,
                     hy_ffn_w2, hy_ffn_b2, hy_ffn_w3, hy_sin_freq, hy_bias, s5_a_re, s5_a_im, s5_log_dt,
                     s5_b_re, s5_b_im, s5_c_re, s5_c_im, s5_d, s5_glu_w, s5_glu_b, w_branch, w_out,
                     norm2_g, w_router, w_gate_e, w_up_e, w_down_e)
    x = jnp.concatenate([x_prompt.reshape(T_GROUP, D_MODEL), x_sample.reshape(T_GROUP, D_MODEL)], axis=0)
    cos, sin_signed = _rope_tables()
    for l in range(DEPTH):
        x = _layer(x, cos, sin_signed, *[w[l] for w in layer_weights])
    return (x[:T_GROUP].reshape(1, SEQ, D_MODEL), x[T_GROUP:].reshape(DEC_BATCH, DEC_SEQ, D_MODEL))
```
